```python
import jax, jax.numpy as jnp
from jax import lax
import numpy as np

D_MODEL = 1024
BATCH = 8
SEQ = 4096
DEPTH = 1

HEAD_DIM = 64
N_HEADS = 8
N_KV_HEADS = 2
GQA_GROUP = N_HEADS // N_KV_HEADS
ATTN_WIDTH = N_HEADS * HEAD_DIM
KV_WIDTH = N_KV_HEADS * HEAD_DIM
POOL_WIDTH = D_MODEL // 2
POOL_WINDOWS = (2, 4, 8, 16)
N_POOL_GROUPS = len(POOL_WINDOWS)
POOL_GROUP_DIM = POOL_WIDTH // N_POOL_GROUPS
MAX_POOL_WINDOW = max(POOL_WINDOWS)
IN_WIDTH = ATTN_WIDTH + 2 * KV_WIDTH + POOL_WIDTH
MIX_WIDTH = ATTN_WIDTH + POOL_WIDTH

WINDOW = 128
BLOCK = 128
ROT_DIM = HEAD_DIM // 4
ROPE_THETA = 500000.0
D_FF = 4 * D_MODEL
N_MOD = 6
EPS = 1e-6
MAX_START = 1024

kernel_name = "hybrid_swa_sink_pool_adaln_block"


def rms_norm(x, g):
    xf = x.astype(jnp.float32)
    y = xf * lax.rsqrt(jnp.mean(xf * xf, axis=-1, keepdims=True) + EPS)
    return (y * g.astype(jnp.float32)).astype(x.dtype)


def apply_partial_rope(x, cos, sin):
    half = ROT_DIM // 2
    x1 = x[..., :half]
    x2 = x[..., half:ROT_DIM]
    cos = cos.astype(x.dtype)
    sin = sin.astype(x.dtype)
    return jnp.concatenate([x1 * cos - x2 * sin, x2 * cos + x1 * sin, x[..., ROT_DIM:]], axis=-1)


def band_mask(n_blocks):
    n = jnp.arange(n_blocks)[:, None, None]
    i = jnp.arange(BLOCK)[None, :, None]
    j = jnp.arange(2 * BLOCK)[None, None, :]
    qpos = n * BLOCK + i
    kpos = (n - 1) * BLOCK + j
    return (kpos <= qpos) & (qpos - kpos < WINDOW) & (kpos >= 0)


def sliding_window_sink_attention(q, k, v, sinks):
    B, S = q.shape[0], q.shape[1]
    nb = S // BLOCK
    qb = q.reshape(B, nb, BLOCK, N_KV_HEADS, GQA_GROUP, HEAD_DIM)
    kb = k.reshape(B, nb, BLOCK, N_KV_HEADS, HEAD_DIM)
    vb = v.reshape(B, nb, BLOCK, N_KV_HEADS, HEAD_DIM)
    pad = ((0, 0), (1, 0), (0, 0), (0, 0), (0, 0))
    k_band = jnp.concatenate([jnp.pad(kb[:, :-1], pad), kb], axis=2)
    v_band = jnp.concatenate([jnp.pad(vb[:, :-1], pad), vb], axis=2)
    scores = jnp.einsum('bnqhgd,bnkhd->bnhgqk', qb, k_band).astype(jnp.float32)
    scores = scores * (HEAD_DIM ** -0.5)
    mask = band_mask(nb)[None, :, None, None]
    scores = jnp.where(mask, scores, jnp.float32(-1e30))
    sink = jnp.broadcast_to(sinks.astype(jnp.float32).reshape(1, 1, N_KV_HEADS, GQA_GROUP, 1, 1),
                            scores.shape[:-1] + (1,))
    probs = jax.nn.softmax(jnp.concatenate([scores, sink], axis=-1), axis=-1)[..., :-1]
    out = jnp.einsum('bnhgqk,bnkhd->bnqhgd', probs.astype(v.dtype), v_band)
    return out.reshape(B, S, ATTN_WIDTH)


def multiscale_pool(u):
    S = u.shape[1]
    uf = u.astype(jnp.float32)
    cs = jnp.cumsum(uf, axis=1)
    cs_pad = jnp.pad(cs, ((0, 0), (MAX_POOL_WINDOW, 0), (0, 0), (0, 0)))
    t = jnp.arange(S)
    means = []
    for g, w in enumerate(POOL_WINDOWS):
        win_sum = cs[:, :, g] - cs_pad[:, MAX_POOL_WINDOW - w:MAX_POOL_WINDOW - w + S, g]
        count = jnp.minimum(t + 1, w).astype(jnp.float32)[None, :, None]
        means.append(win_sum / count)
    mean = jnp.stack(means, axis=2)
    return (mean - uf).astype(u.dtype)


def setup_inputs(seed: int = 0) -> dict:
    key = jax.random.key(seed)
    ks = jax.random.split(key, 20)
    f32 = jnp.float32
    nrm = lambda k, shape, s: jax.random.normal(k, shape, f32) * s
    x = nrm(ks[0], (BATCH, SEQ, D_MODEL), 1.0)
    c = nrm(ks[1], (BATCH, D_MODEL), 1.0)
    start = jax.random.randint(ks[2], (BATCH, 1), 0, MAX_START, dtype=jnp.int32)
    positions = (start + jnp.arange(SEQ, dtype=jnp.int32)[None, :]).astype(jnp.int32)
    return {
        "x": x,
        "c": c,
        "positions": positions,
        "w_ada": nrm(ks[3], (DEPTH, D_MODEL, N_MOD * D_MODEL), 0.5 * D_MODEL ** -0.5),
        "b_ada": nrm(ks[4], (DEPTH, N_MOD * D_MODEL), 0.01),
        "norm1_g": 1.0 + nrm(ks[5], (DEPTH, D_MODEL), 0.05),
        "w_in": nrm(ks[6], (DEPTH, D_MODEL, IN_WIDTH), D_MODEL ** -0.5),
        "q_norm_g": 1.0 + nrm(ks[7], (DEPTH, HEAD_DIM), 0.05),
        "k_norm_g": 1.0 + nrm(ks[8], (DEPTH, HEAD_DIM), 0.05),
        "attn_sinks": nrm(ks[9], (DEPTH, N_HEADS), 0.5),
        "pool_w": nrm(ks[10], (DEPTH, N_POOL_GROUPS, POOL_GROUP_DIM, POOL_GROUP_DIM), POOL_GROUP_DIM ** -0.5),
        "pool_scale": 1.0 + nrm(ks[11], (DEPTH, POOL_WIDTH), 0.1),
        "w_o": nrm(ks[12], (DEPTH, MIX_WIDTH, D_MODEL), MIX_WIDTH ** -0.5),
        "norm2_g": 1.0 + nrm(ks[13], (DEPTH, D_MODEL), 0.05),
        "w_mlp_in": nrm(ks[14], (DEPTH, D_MODEL, D_FF), D_MODEL ** -0.5),
        "w_mlp_out": nrm(ks[15], (DEPTH, D_FF, D_MODEL), D_FF ** -0.5),
    }


def reference(x, c, positions, w_ada, b_ada, norm1_g, w_in, q_norm_g, k_norm_g, attn_sinks,
              pool_w, pool_scale, w_o, norm2_g, w_mlp_in, w_mlp_out):
    B, S, D = x.shape
    inv_freq = ROPE_THETA ** (-jnp.arange(0, ROT_DIM, 2, dtype=jnp.float32) / ROT_DIM)
    ang = positions.astype(jnp.float32)[..., None] * inv_freq
    cos, sin = jnp.cos(ang), jnp.sin(ang)
    cos_q, sin_q = cos[:, :, None, None, :], sin[:, :, None, None, :]
    cos_k, sin_k = cos[:, :, None, :], sin[:, :, None, :]
    c_act = jax.nn.silu(c)
    h_res = x
    for l in range(DEPTH):
        mod = (c_act @ w_ada[l] + b_ada[l])[:, None, :]
        sh1, sc1, g1, sh2, sc2, g2 = jnp.split(mod, N_MOD, axis=-1)

        h = rms_norm(h_res, norm1_g[l]) * (1.0 + sc1) + sh1
        proj = h @ w_in[l]
        q = proj[..., :ATTN_WIDTH].reshape(B, S, N_KV_HEADS, GQA_GROUP, HEAD_DIM)
        k = proj[..., ATTN_WIDTH:ATTN_WIDTH + KV_WIDTH].reshape(B, S, N_KV_HEADS, HEAD_DIM)
        v = proj[..., ATTN_WIDTH + KV_WIDTH:ATTN_WIDTH + 2 * KV_WIDTH].reshape(B, S, N_KV_HEADS, HEAD_DIM)
        u = proj[..., ATTN_WIDTH + 2 * KV_WIDTH:].reshape(B, S, N_POOL_GROUPS, POOL_GROUP_DIM)

        q = apply_partial_rope(rms_norm(q, q_norm_g[l]), cos_q, sin_q)
        k = apply_partial_rope(rms_norm(k, k_norm_g[l]), cos_k, sin_k)
        attn_out = sliding_window_sink_attention(q, k, v, attn_sinks[l])

        pooled = multiscale_pool(u)
        pool_out = jnp.einsum('bsgc,gcd->bsgd', pooled, pool_w[l])
        pool_out = pool_out.reshape(B, S, POOL_WIDTH) * pool_scale[l]

        mixed = jnp.concatenate([attn_out, pool_out], axis=-1) @ w_o[l]
        h_res = h_res + g1 * mixed

        h2 = rms_norm(h_res, norm2_g[l]) * (1.0 + sc2) + sh2
        ff = jnp.square(jax.nn.relu(h2 @ w_mlp_in[l])) @ w_mlp_out[l]
        h_res = h_res + g2 * ff
    return h_res
```

```python
import functools

import jax
import jax.numpy as jnp
from jax import lax
from jax.experimental import pallas as pl
from jax.experimental.pallas import tpu as pltpu

F32 = jnp.float32
BF16 = jnp.bfloat16

HEAD_DIM = 64
N_HEADS = 8
N_KV_HEADS = 2
GQA_GROUP = N_HEADS // N_KV_HEADS
ATTN_WIDTH = N_HEADS * HEAD_DIM
KV_WIDTH = N_KV_HEADS * HEAD_DIM
QKV_WIDTH = ATTN_WIDTH + 2 * KV_WIDTH
POOL_WINDOWS = (2, 4, 8, 16)
POOL_GROUP_DIM = 128
POOL_WIDTH = len(POOL_WINDOWS) * POOL_GROUP_DIM
MAX_POOL_WINDOW = max(POOL_WINDOWS)
BLOCK = 128
ROT_HALF = 8
ROPE_THETA = 500000.0
N_MOD = 6
EPS = 1e-6
MASK_VALUE = -1e30

V7X_VMEM_BYTES = 64 * 1024 * 1024
VMEM_LIMIT_BYTES = V7X_VMEM_BYTES - 8 * 1024 * 1024

MIXER_TOKENS = 512
MLP_TOKENS = 512
MLP_FF_CHUNK = 1024


def _modulated_rms_norm(x, gain, scale, shift):
    y = x * lax.rsqrt(jnp.mean(x * x, axis=-1, keepdims=True) + EPS)
    return (y * gain) * (1.0 + scale) + shift


def _ada_kernel(c_ref, w_ref, b_ref, o_ref):
    c = c_ref[...]
    c_act = c * jax.nn.sigmoid(c)
    o_ref[0] = jnp.dot(c_act.astype(BF16), w_ref[...].astype(BF16),
                       preferred_element_type=F32) + b_ref[...]


def _ada_modulation(c, w_ada, b_ada):
    batch, d = c.shape
    return pl.pallas_call(
        _ada_kernel,
        grid=(N_MOD,),
        in_specs=[
            pl.BlockSpec((batch, d), lambda n: (0, 0)),
            pl.BlockSpec((d, d), lambda n: (0, n)),
            pl.BlockSpec((1, d), lambda n: (0, n)),
        ],
        out_specs=pl.BlockSpec((1, batch, d), lambda n: (n, 0, 0)),
        out_shape=jax.ShapeDtypeStruct((N_MOD, batch, d), F32),
        compiler_params=pltpu.CompilerParams(
            dimension_semantics=("arbitrary",), vmem_limit_bytes=VMEM_LIMIT_BYTES),
        name="ada_modulation",
    )(c, w_ada, b_ada.reshape(1, N_MOD * d))


def _norm_rope_t(x_t, gain_b, cos_t, sin_t):
    n_tok = x_t.shape[1]
    x3 = x_t.reshape(x_t.shape[0] // HEAD_DIM, HEAD_DIM, n_tok)
    ss = jnp.sum(x3 * x3, axis=1, keepdims=True)
    xn = (x3 * lax.rsqrt(ss * (1.0 / HEAD_DIM) + EPS)) * gain_b[None]
    x1 = xn[:, 0:ROT_HALF]
    x2 = xn[:, ROT_HALF:2 * ROT_HALF]
    r1 = x1 * cos_t - x2 * sin_t
    r2 = x2 * cos_t + x1 * sin_t
    out = jnp.concatenate([r1, r2, xn[:, 2 * ROT_HALF:]], axis=1)
    return out.reshape(x_t.shape)


def _mixer_kernel(sinks_ref, x_ref, mod_ref, pos_ref, gain_ref, wqkv_t_ref, wu_ref, gq_ref, gk_ref,
                  freq_ref, poolw_ref, pscale_ref, woa_ref, wop_ref, o_ref,
                  q_t_s, k_s, v_t_s, u_s, attn_t_s):
    ts = x_ref.shape[1]
    t = pl.program_id(1)

    @pl.when(t == 0)
    def _():
        k_s[0:BLOCK, :] = jnp.zeros((BLOCK, KV_WIDTH), BF16)
        v_t_s[:, 0:BLOCK] = jnp.zeros((KV_WIDTH, BLOCK), BF16)
        u_s[0:MAX_POOL_WINDOW, :] = jnp.zeros((MAX_POOL_WINDOW, POOL_WIDTH), F32)

    x = x_ref[0]
    mod = mod_ref[0]
    h = _modulated_rms_norm(x, gain_ref[...], mod[1:2], mod[0:1]).astype(BF16)

    qkv_t = lax.dot_general(wqkv_t_ref[...], h, (((1,), (1,)), ((), ())), preferred_element_type=F32)
    u = jnp.dot(h, wu_ref[...], preferred_element_type=F32)

    ang = pos_ref[0].astype(F32) * freq_ref[...]
    cos_t, sin_t = jnp.cos(ang), jnp.sin(ang)

    q_t_s[...] = _norm_rope_t(qkv_t[0:ATTN_WIDTH], gq_ref[...], cos_t, sin_t).astype(BF16)
    k_t = _norm_rope_t(qkv_t[ATTN_WIDTH:ATTN_WIDTH + KV_WIDTH], gk_ref[...], cos_t, sin_t)
    k_s[BLOCK:BLOCK + ts, :] = k_t.T.astype(BF16)
    v_t_s[:, BLOCK:BLOCK + ts] = qkv_t[ATTN_WIDTH + KV_WIDTH:QKV_WIDTH].astype(BF16)

    r_idx = lax.broadcasted_iota(jnp.int32, (2 * BLOCK, GQA_GROUP * BLOCK), 0)
    i_idx = lax.broadcasted_iota(jnp.int32, (2 * BLOCK, GQA_GROUP * BLOCK), 1) % BLOCK
    bias = jnp.where((r_idx > i_idx) & (r_idx <= i_idx + BLOCK), 0.0, MASK_VALUE).astype(F32)
    bias_first = jnp.where((r_idx < BLOCK) & (t == 0), MASK_VALUE, bias)

    ones_rows = jnp.ones((16, 2 * BLOCK), BF16)
    zero_rows = jnp.zeros((HEAD_DIM, GQA_GROUP * BLOCK), BF16)
    for j in range(ts // BLOCK):
        k_band = k_s[BLOCK * j:BLOCK * (j + 2), :]
        for kv in range(N_KV_HEADS):
            heads = [kv * GQA_GROUP + g for g in range(GQA_GROUP)]
            q_grp = jnp.concatenate(
                [q_t_s[HEAD_DIM * hd:HEAD_DIM * (hd + 1), BLOCK * j:BLOCK * (j + 1)] for hd in heads], axis=1)
            q_aug = jnp.concatenate([q_grp, zero_rows] if kv == 0 else [zero_rows, q_grp], axis=0)
            s = jnp.dot(k_band, q_aug, preferred_element_type=F32)
            s = s + (bias_first if j == 0 else bias)
            sink = jnp.concatenate([jnp.full((1, BLOCK), sinks_ref[hd], F32) for hd in heads], axis=1)
            m = jnp.maximum(jnp.max(s, axis=0, keepdims=True), sink)
            p = jnp.exp(s - m).astype(BF16)
            v_aug = jnp.concatenate(
                [v_t_s[HEAD_DIM * kv:HEAD_DIM * (kv + 1), BLOCK * j:BLOCK * (j + 2)], ones_rows], axis=0)
            o_aug = jnp.dot(v_aug, p, preferred_element_type=F32)
            denom = o_aug[HEAD_DIM:HEAD_DIM + 1] + jnp.exp(sink - m)
            o_t = (o_aug[0:HEAD_DIM] * (1.0 / denom)).astype(BF16)
            for g, hd in enumerate(heads):
                attn_t_s[HEAD_DIM * hd:HEAD_DIM * (hd + 1), BLOCK * j:BLOCK * (j + 1)] = (
                    o_t[:, BLOCK * g:BLOCK * (g + 1)])

    u_s[MAX_POOL_WINDOW:MAX_POOL_WINDOW + ts, :] = u
    u_ext = u_s[...]
    seq_pos = t * ts + lax.broadcasted_iota(jnp.int32, (ts, 1), 0)
    pool_parts = []
    for g, w in enumerate(POOL_WINDOWS):
        lanes = slice(POOL_GROUP_DIM * g, POOL_GROUP_DIM * (g + 1))
        win = u_ext[:, lanes]
        shift = 1
        while shift < w:
            win = win + pltpu.roll(win, shift, axis=0)
            shift *= 2
        inv_count = 1.0 / jnp.minimum(seq_pos + 1, w).astype(F32)
        pooled = win[MAX_POOL_WINDOW:] * inv_count - u[:, lanes]
        part = jnp.dot(pooled.astype(BF16), poolw_ref[g], preferred_element_type=F32)
        pool_parts.append((part * pscale_ref[:, lanes]).astype(BF16))
    pool_out = jnp.concatenate(pool_parts, axis=1)

    mixed = lax.dot_general(attn_t_s[...], woa_ref[...], (((0,), (0,)), ((), ())),
                            preferred_element_type=F32)
    mixed = mixed + jnp.dot(pool_out, wop_ref[...], preferred_element_type=F32)
    o_ref[0] = x + mod[2:3] * mixed

    k_s[0:BLOCK, :] = k_s[ts:ts + BLOCK, :]
    v_t_s[:, 0:BLOCK] = v_t_s[:, ts:ts + BLOCK]
    u_s[0:MAX_POOL_WINDOW, :] = u_s[ts:ts + MAX_POOL_WINDOW, :]


def _mixer(x, mod, positions, norm_g, w_in, q_norm_g, k_norm_g, sinks, pool_w, pool_scale, w_o):
    batch, seq, d = x.shape
    ts = MIXER_TOKENS
    assert seq % ts == 0 and ts % BLOCK == 0
    wqkv_t = w_in[:, :QKV_WIDTH].T.astype(BF16)
    wu = w_in[:, QKV_WIDTH:].astype(BF16)
    gq = jnp.broadcast_to((q_norm_g * (HEAD_DIM ** -0.5))[:, None], (HEAD_DIM, ts)).astype(F32)
    gk = jnp.broadcast_to(k_norm_g[:, None], (HEAD_DIM, ts)).astype(F32)
    inv_freq = ROPE_THETA ** (-jnp.arange(0, 2 * ROT_HALF, 2, dtype=F32) / (2 * ROT_HALF))
    freq = jnp.broadcast_to(inv_freq[:, None], (ROT_HALF, ts))
    const = lambda *shape: pl.BlockSpec(shape, lambda b, t: (0,) * len(shape))
    return pl.pallas_call(
        _mixer_kernel,
        grid=(batch, seq // ts),
        in_specs=[
            pl.BlockSpec(memory_space=pltpu.SMEM),
            pl.BlockSpec((1, ts, d), lambda b, t: (b, t, 0)),
            pl.BlockSpec((1, N_MOD, d), lambda b, t: (b, 0, 0)),
            pl.BlockSpec((1, 1, ts), lambda b, t: (b, 0, t)),
            const(1, d),
            const(QKV_WIDTH, d),
            const(d, POOL_WIDTH),
            const(HEAD_DIM, ts),
            const(HEAD_DIM, ts),
            const(ROT_HALF, ts),
            const(len(POOL_WINDOWS), POOL_GROUP_DIM, POOL_GROUP_DIM),
            const(1, POOL_WIDTH),
            const(ATTN_WIDTH, d),
            const(POOL_WIDTH, d),
        ],
        out_specs=pl.BlockSpec((1, ts, d), lambda b, t: (b, t, 0)),
        out_shape=jax.ShapeDtypeStruct((batch, seq, d), F32),
        scratch_shapes=[
            pltpu.VMEM((ATTN_WIDTH, ts), BF16),
            pltpu.VMEM((BLOCK + ts, KV_WIDTH), BF16),
            pltpu.VMEM((KV_WIDTH, BLOCK + ts), BF16),
            pltpu.VMEM((MAX_POOL_WINDOW + ts, POOL_WIDTH), F32),
            pltpu.VMEM((ATTN_WIDTH, ts), BF16),
        ],
        compiler_params=pltpu.CompilerParams(
            dimension_semantics=("arbitrary", "arbitrary"), vmem_limit_bytes=VMEM_LIMIT_BYTES),
        name="token_mixer",
    )(sinks.astype(F32), x, mod, positions.reshape(batch, 1, seq), norm_g.reshape(1, d), wqkv_t, wu,
      gq, gk, freq, pool_w.astype(BF16), pool_scale.reshape(1, POOL_WIDTH),
      w_o[:ATTN_WIDTH].astype(BF16), w_o[ATTN_WIDTH:].astype(BF16))


def _mlp_kernel(x_ref, mod_ref, gain_ref, w1_ref, w2_ref, o_ref):
    x = x_ref[0]
    mod = mod_ref[0]
    h = _modulated_rms_norm(x, gain_ref[...], mod[4:5], mod[3:4]).astype(BF16)
    acc = None
    for c in range(w1_ref.shape[0]):
        a = jnp.maximum(jnp.dot(h, w1_ref[c], preferred_element_type=F32), 0.0)
        part = jnp.dot((a * a).astype(BF16), w2_ref[c], preferred_element_type=F32)
        acc = part if acc is None else acc + part
    o_ref[0] = x + mod[5:6] * acc


def _mlp(x, mod, norm_g, w1, w2):
    batch, seq, d = x.shape
    d_ff = w1.shape[1]
    tm, fc = MLP_TOKENS, MLP_FF_CHUNK
    assert seq % tm == 0 and d_ff % fc == 0
    w1c = w1.astype(BF16).reshape(d, d_ff // fc, fc).transpose(1, 0, 2)
    w2c = w2.astype(BF16).reshape(d_ff // fc, fc, d)
    return pl.pallas_call(
        _mlp_kernel,
        grid=(batch, seq // tm),
        in_specs=[
            pl.BlockSpec((1, tm, d), lambda b, t: (b, t, 0)),
            pl.BlockSpec((1, N_MOD, d), lambda b, t: (b, 0, 0)),
            pl.BlockSpec((1, d), lambda b, t: (0, 0)),
            pl.BlockSpec(w1c.shape, lambda b, t: (0, 0, 0), pipeline_mode=pl.Buffered(1)),
            pl.BlockSpec(w2c.shape, lambda b, t: (0, 0, 0), pipeline_mode=pl.Buffered(1)),
        ],
        out_specs=pl.BlockSpec((1, tm, d), lambda b, t: (b, t, 0)),
        out_shape=jax.ShapeDtypeStruct((batch, seq, d), F32),
        compiler_params=pltpu.CompilerParams(
            dimension_semantics=("arbitrary", "arbitrary"), vmem_limit_bytes=VMEM_LIMIT_BYTES),
        name="relu2_mlp",
    )(x, mod, norm_g.reshape(1, d), w1c, w2c)


def kernel(x, c, positions, w_ada, b_ada, norm1_g, w_in, q_norm_g, k_norm_g, attn_sinks,
           pool_w, pool_scale, w_o, norm2_g, w_mlp_in, w_mlp_out):
    h_res = x
    for l in range(w_ada.shape[0]):
        mod = _ada_modulation(c, w_ada[l], b_ada[l]).transpose(1, 0, 2)
        h_res = _mixer(h_res, mod, positions, norm1_g[l], w_in[l], q_norm_g[l], k_norm_g[l],
                       attn_sinks[l], pool_w[l], pool_scale[l], w_o[l])
        h_res = _mlp(h_res, mod, norm2_g[l], w_mlp_in[l], w_mlp_out[l])
    return h_res
```

```python
import functools

import jax
import jax.numpy as jnp
from jax import lax
from jax.experimental import pallas as pl
from jax.experimental.pallas import tpu as pltpu

F32 = jnp.float32
BF16 = jnp.bfloat16

HEAD_DIM = 64
N_HEADS = 8
N_KV_HEADS = 2
GQA_GROUP = N_HEADS // N_KV_HEADS
ATTN_WIDTH = N_HEADS * HEAD_DIM
KV_WIDTH = N_KV_HEADS * HEAD_DIM
QKV_WIDTH = ATTN_WIDTH + 2 * KV_WIDTH
POOL_WINDOWS = (2, 4, 8, 16)
POOL_GROUP_DIM = 128
POOL_WIDTH = len(POOL_WINDOWS) * POOL_GROUP_DIM
MAX_POOL_WINDOW = max(POOL_WINDOWS)
BLOCK = 128
ROT_HALF = 8
ROPE_THETA = 500000.0
N_MOD = 6
EPS = 1e-6
MASK_VALUE = -1e30

V7X_VMEM_BYTES = 64 * 1024 * 1024
VMEM_LIMIT_BYTES = V7X_VMEM_BYTES - 8 * 1024 * 1024

MIXER_TOKENS = 512
PRODUCE_CHUNK = 256
QKV_ROWS = 256
OUT_COLS = 256
MIXER_ORDER = "wawa" "pppa" "pa" "ppa" "ppa" "pwa" "pa" "pwa" "pa" "pa"
U_COLS = 512
PRODUCE_ITEMS = "n0 n1 u0 m0 m1 m2 g0 q0 g1 q1 g2 k0 g3"
MLP_TOKENS = 1024
MLP_FF_CHUNK = 1024


def _modulated_rms_norm(x, gain, scale, shift):
    y = x * lax.rsqrt(jnp.mean(x * x, axis=-1, keepdims=True) + EPS)
    return y * (gain * (1.0 + scale)) + shift


def _ada_kernel(c_ref, w_ref, b_ref, o_ref):
    c = c_ref[...]
    c_act = c * jax.nn.sigmoid(c)
    o_ref[0] = jnp.dot(c_act.astype(BF16), w_ref[...].astype(BF16),
                       preferred_element_type=F32) + b_ref[...]


def _ada_modulation(c, w_ada, b_ada):
    batch, d = c.shape
    return pl.pallas_call(
        _ada_kernel,
        grid=(N_MOD,),
        in_specs=[
            pl.BlockSpec((batch, d), lambda n: (0, 0)),
            pl.BlockSpec((d, d), lambda n: (0, n)),
            pl.BlockSpec((1, d), lambda n: (0, n)),
        ],
        out_specs=pl.BlockSpec((1, batch, d), lambda n: (n, 0, 0)),
        out_shape=jax.ShapeDtypeStruct((N_MOD, batch, d), F32),
        compiler_params=pltpu.CompilerParams(
            dimension_semantics=("arbitrary",), vmem_limit_bytes=VMEM_LIMIT_BYTES),
        name="ada_modulation",
    )(c, w_ada, b_ada.reshape(1, N_MOD * d))


def _norm_rope_t(x_t, gain_b, cos_t, sin_t):
    n_tok = x_t.shape[1]
    x3 = x_t.reshape(x_t.shape[0] // HEAD_DIM, HEAD_DIM, n_tok)
    ss = jnp.sum(x3 * x3, axis=1, keepdims=True)
    xn = (x3 * lax.rsqrt(ss * (1.0 / HEAD_DIM) + EPS)) * gain_b[None]
    x1 = xn[:, 0:ROT_HALF]
    x2 = xn[:, ROT_HALF:2 * ROT_HALF]
    r1 = x1 * cos_t - x2 * sin_t
    r2 = x2 * cos_t + x1 * sin_t
    out = jnp.concatenate([r1, r2, xn[:, 2 * ROT_HALF:]], axis=1)
    return out.reshape(x_t.shape)


def _band_bias():
    shape = (2 * BLOCK, GQA_GROUP * BLOCK)
    r_idx = lax.broadcasted_iota(jnp.int32, shape, 0)
    i_idx = lax.broadcasted_iota(jnp.int32, shape, 1) % BLOCK
    bias = jnp.where((r_idx > i_idx) & (r_idx <= i_idx + BLOCK), 0.0, MASK_VALUE).astype(F32)
    return bias, jnp.where(r_idx < BLOCK, MASK_VALUE, bias)


def _attend_stage(slot, out_slot, first_tile, sinks_ref, q_t_s, k_s, v_t_s, attn_s, bias_s):
    ts = q_t_s.shape[2]
    bias = bias_s[0]
    bias_first = bias_s[first_tile.astype(jnp.int32)]
    ones_rows = jnp.ones((16, 2 * BLOCK), BF16)
    zero_rows = jnp.zeros((HEAD_DIM, GQA_GROUP * BLOCK), BF16)
    pairs = [(j, kv) for j in range(ts // BLOCK) for kv in range(N_KV_HEADS)]

    def scores(j, kv):
        k_band = k_s[slot, BLOCK * j:BLOCK * (j + 2), :]
        q_grp = jnp.concatenate(
            [q_t_s[slot, HEAD_DIM * hd:HEAD_DIM * (hd + 1), BLOCK * j:BLOCK * (j + 1)]
             for hd in range(kv * GQA_GROUP, (kv + 1) * GQA_GROUP)], axis=1)
        q_aug = jnp.concatenate([q_grp, zero_rows] if kv == 0 else [zero_rows, q_grp], axis=0)
        return jnp.dot(k_band, q_aug, preferred_element_type=F32)

    def softmax(j, kv, s):
        s = s + (bias_first if j == 0 else bias)
        sink = jnp.concatenate([jnp.full((1, BLOCK), sinks_ref[hd], F32)
                                for hd in range(kv * GQA_GROUP, (kv + 1) * GQA_GROUP)], axis=1)
        m = jnp.maximum(jnp.max(s, axis=0, keepdims=True), sink)
        return jnp.exp(s - m).astype(BF16), jnp.exp(sink - m)

    def weighted_values(j, kv, p):
        v_aug = jnp.concatenate(
            [v_t_s[slot, HEAD_DIM * kv:HEAD_DIM * (kv + 1), BLOCK * j:BLOCK * (j + 2)], ones_rows], axis=0)
        return jnp.dot(v_aug, p, preferred_element_type=F32)

    def finish(j, kv, o_aug, sink_weight):
        denom = o_aug[HEAD_DIM:HEAD_DIM + 1] + sink_weight
        o_t = o_aug[0:HEAD_DIM] * (1.0 / denom)
        for g in range(0, GQA_GROUP, 2):
            two_heads = jnp.concatenate([o_t[:, BLOCK * g:BLOCK * (g + 1)],
                                         o_t[:, BLOCK * (g + 1):BLOCK * (g + 2)]], axis=0)
            lane0 = HEAD_DIM * (kv * GQA_GROUP + g)
            attn_s[out_slot, BLOCK * j:BLOCK * (j + 1), lane0:lane0 + 2 * HEAD_DIM] = (
                two_heads.T.astype(BF16))

    s_q, p_q, o_q = {}, {}, {}
    for n in range(len(pairs) + 3):
        if n < len(pairs):
            s_q[n] = scores(*pairs[n])
        if 0 <= n - 2 < len(pairs):
            p, sink_weight = p_q.pop(n - 2)
            o_q[n - 2] = weighted_values(*pairs[n - 2], p), sink_weight
        if 0 <= n - 1 < len(pairs):
            p_q[n - 1] = softmax(*pairs[n - 1], s_q.pop(n - 1))
        if 0 <= n - 3 < len(pairs):
            finish(*pairs[n - 3], *o_q.pop(n - 3))
        yield


def _project_stage(attn_slot, pool_slot, x_ref, mod_ref, woa_ref, wop_ref, o_ref, attn_s, pool_s):
    d = x_ref.shape[2]
    gate = mod_ref[0][2:3]
    for n in range(d // OUT_COLS):
        cols = slice(OUT_COLS * n, OUT_COLS * (n + 1))
        mixed = jnp.dot(attn_s[attn_slot], woa_ref[:, cols], preferred_element_type=F32)
        mixed = mixed + jnp.dot(pool_s[pool_slot], wop_ref[:, cols], preferred_element_type=F32)
        o_ref[0, :, cols] = x_ref[0, :, cols] + gate[:, cols] * mixed
        yield


def _produce_stage(slot, pool_slot, first_tile, seq_start, x_ref, mod_ref, pos_ref, gain_ref, wqkv_t_ref,
                   wu_ref, gq_ref, gk_ref, freq_ref, poolw_ref, pscale_ref,
                   q_t_s, k_s, v_t_s, pool_s, k_tail_s, v_tail_s, u_s):
    ts = x_ref.shape[1]
    ch = PRODUCE_CHUNK
    mod = mod_ref[0]
    gain_row = gain_ref[...] * (1.0 + mod[1:2])
    shift_row = mod[0:1]

    k_s[slot, 0:BLOCK, :] = jnp.where(first_tile, jnp.zeros_like(k_tail_s), k_tail_s[...])
    v_t_s[slot, :, 0:BLOCK] = jnp.where(first_tile, jnp.zeros_like(v_tail_s), v_tail_s[...])
    u_tail = u_s[ts:ts + MAX_POOL_WINDOW, :]
    u_s[0:MAX_POOL_WINDOW, :] = jnp.where(first_tile, jnp.zeros_like(u_tail), u_tail)

    val = {}

    def norm_rows(c):
        x = x_ref[0, ch * c:ch * (c + 1), :]
        y = x * lax.rsqrt(jnp.mean(x * x, axis=-1, keepdims=True) + EPS)
        val["h", c] = (y * gain_row + shift_row).astype(BF16)
        if c == ts // ch - 1:
            val["h"] = jnp.concatenate([val.pop(("h", n)) for n in range(ts // ch)], axis=0)

    def pool_input(n):
        lanes = slice(U_COLS * n, U_COLS * (n + 1))
        val["u", n] = jnp.dot(val["h"], wu_ref[:, lanes], preferred_element_type=F32)
        u_s[MAX_POOL_WINDOW:MAX_POOL_WINDOW + ts, lanes] = val["u", n]

    def qkv_rows(n):
        val["qkv", n] = lax.dot_general(wqkv_t_ref[QKV_ROWS * n:QKV_ROWS * (n + 1), :], val["h"],
                                        (((1,), (1,)), ((), ())), preferred_element_type=F32)

    def pool_group(g):
        w = POOL_WINDOWS[g]
        lanes = slice(POOL_GROUP_DIM * g, POOL_GROUP_DIM * (g + 1))
        win = u_s[:, lanes]
        shift = 1
        while shift < w:
            win = win + pltpu.roll(win, shift, axis=0)
            shift *= 2
        seq_pos = seq_start + lax.broadcasted_iota(jnp.int32, (ts, 1), 0)
        inv_count = 1.0 / jnp.minimum(seq_pos + 1, w).astype(F32)
        groups_per_piece = U_COLS // POOL_GROUP_DIM
        u_g = val["u", g // groups_per_piece][:, POOL_GROUP_DIM * (g % groups_per_piece):
                                              POOL_GROUP_DIM * (g % groups_per_piece + 1)]
        pooled = win[MAX_POOL_WINDOW:] * inv_count - u_g
        part = jnp.dot(pooled.astype(BF16), poolw_ref[g], preferred_element_type=F32)
        pool_s[pool_slot, :, lanes] = (part * pscale_ref[:, lanes]).astype(BF16)

    def rotary():
        if "cos" not in val:
            ang = pos_ref[0].astype(F32) * freq_ref[...]
            val["cos"], val["sin"] = jnp.cos(ang), jnp.sin(ang)
        return val["cos"], val["sin"]

    def q_heads(n):
        q_t_s[slot, QKV_ROWS * n:QKV_ROWS * (n + 1), :] = _norm_rope_t(
            val.pop(("qkv", n)), gq_ref[...], *rotary()).astype(BF16)

    def k_and_v():
        kv_t = val.pop(("qkv", ATTN_WIDTH // QKV_ROWS))
        k_t = _norm_rope_t(kv_t[0:KV_WIDTH], gk_ref[...], *rotary())
        k_new = k_t.T.astype(BF16)
        v_new = kv_t[KV_WIDTH:2 * KV_WIDTH].astype(BF16)
        k_s[slot, BLOCK:BLOCK + ts, :] = k_new
        v_t_s[slot, :, BLOCK:BLOCK + ts] = v_new
        k_tail_s[...] = k_new[ts - BLOCK:ts, :]
        v_tail_s[...] = v_new[:, ts - BLOCK:ts]

    items = {"n": norm_rows, "u": pool_input, "m": qkv_rows, "g": pool_group, "q": q_heads,
             "k": lambda _: k_and_v()}
    for name in PRODUCE_ITEMS.split():
        items[name[0]](int(name[1:]))
        yield


def _interleave(order, **stages):
    for key in order:
        next(stages[key], None)
    for gen in stages.values():
        for _ in gen:
            pass


def _mixer_kernel(tiles_per_seq, sinks_ref, xp_ref, xw_ref, modp_ref, modw_ref, pos_ref, gain_ref,
                  wqkv_t_ref, wu_ref, gq_ref, gk_ref, freq_ref, poolw_ref, pscale_ref, woa_ref, wop_ref,
                  o_ref, q_t_s, k_s, v_t_s, pool_s, attn_s, bias_s, k_tail_s, v_tail_s, u_s):
    ts = xp_ref.shape[1]
    i = pl.program_id(0)
    n_tiles = pl.num_programs(0) - 2
    even, odd = i % 2, 1 - i % 2
    produce_tile = jnp.minimum(i, n_tiles - 1) % tiles_per_seq
    attend_tile = jnp.clip(i - 1, 0, n_tiles - 1) % tiles_per_seq

    @pl.when(i == 0)
    def _():
        q_t_s[1] = jnp.zeros(q_t_s.shape[1:], BF16)
        k_s[1] = jnp.zeros(k_s.shape[1:], BF16)
        v_t_s[1] = jnp.zeros(v_t_s.shape[1:], BF16)
        attn_s[1] = jnp.zeros(attn_s.shape[1:], BF16)
        pool_s[1] = jnp.zeros(pool_s.shape[1:], BF16)
        pool_s[2] = jnp.zeros(pool_s.shape[1:], BF16)
        k_tail_s[...] = jnp.zeros(k_tail_s.shape, BF16)
        v_tail_s[...] = jnp.zeros(v_tail_s.shape, BF16)
        u_s[...] = jnp.zeros(u_s.shape, F32)
        bias, bias_first = _band_bias()
        bias_s[0] = bias
        bias_s[1] = bias_first

    produce = _produce_stage(even, lax.rem(i, 3), produce_tile == 0, produce_tile * ts, xp_ref, modp_ref,
                             pos_ref, gain_ref, wqkv_t_ref, wu_ref, gq_ref, gk_ref, freq_ref, poolw_ref,
                             pscale_ref, q_t_s, k_s, v_t_s, pool_s, k_tail_s, v_tail_s, u_s)
    attend = _attend_stage(odd, even, attend_tile == 0, sinks_ref, q_t_s, k_s, v_t_s, attn_s, bias_s)
    project = _project_stage(odd, lax.rem(i + 1, 3), xw_ref, modw_ref, woa_ref, wop_ref, o_ref,
                             attn_s, pool_s)
    _interleave(MIXER_ORDER, p=produce, a=attend, w=project)


def _mixer(x, mod, positions, norm_g, w_in, q_norm_g, k_norm_g, sinks, pool_w, pool_scale, w_o):
    batch, seq, d = x.shape
    ts = MIXER_TOKENS
    assert seq % ts == 0 and ts % BLOCK == 0 and ts % PRODUCE_CHUNK == 0
    assert d % OUT_COLS == 0 and ATTN_WIDTH % QKV_ROWS == 0 and QKV_ROWS == 2 * KV_WIDTH
    tiles_per_seq = seq // ts
    n_tiles = batch * tiles_per_seq
    wqkv_t = w_in[:, :QKV_WIDTH].T.astype(BF16)
    wu = w_in[:, QKV_WIDTH:].astype(BF16)
    gq = jnp.broadcast_to((q_norm_g * (HEAD_DIM ** -0.5))[:, None], (HEAD_DIM, ts)).astype(F32)
    gk = jnp.broadcast_to(k_norm_g[:, None], (HEAD_DIM, ts)).astype(F32)
    inv_freq = ROPE_THETA ** (-jnp.arange(0, 2 * ROT_HALF, 2, dtype=F32) / (2 * ROT_HALF))
    freq = jnp.broadcast_to(inv_freq[:, None], (ROT_HALF, ts))

    def produced(i):
        return jnp.minimum(i, n_tiles - 1) // tiles_per_seq, jnp.minimum(i, n_tiles - 1) % tiles_per_seq

    def projected(i):
        return jnp.maximum(i - 2, 0) // tiles_per_seq, jnp.maximum(i - 2, 0) % tiles_per_seq

    const = lambda *shape: pl.BlockSpec(shape, lambda i: (0,) * len(shape))
    return pl.pallas_call(
        functools.partial(_mixer_kernel, tiles_per_seq),
        grid=(n_tiles + 2,),
        in_specs=[
            pl.BlockSpec(memory_space=pltpu.SMEM),
            pl.BlockSpec((1, ts, d), lambda i: (*produced(i), 0)),
            pl.BlockSpec((1, ts, d), lambda i: (*projected(i), 0)),
            pl.BlockSpec((1, N_MOD, d), lambda i: (produced(i)[0], 0, 0)),
            pl.BlockSpec((1, N_MOD, d), lambda i: (projected(i)[0], 0, 0)),
            pl.BlockSpec((1, 1, ts), lambda i: (produced(i)[0], 0, produced(i)[1])),
            const(1, d),
            const(QKV_WIDTH, d),
            const(d, POOL_WIDTH),
            const(HEAD_DIM, ts),
            const(HEAD_DIM, ts),
            const(ROT_HALF, ts),
            const(len(POOL_WINDOWS), POOL_GROUP_DIM, POOL_GROUP_DIM),
            const(1, POOL_WIDTH),
            const(ATTN_WIDTH, d),
            const(POOL_WIDTH, d),
        ],
        out_specs=pl.BlockSpec((1, ts, d), lambda i: (*projected(i), 0)),
        out_shape=jax.ShapeDtypeStruct((batch, seq, d), F32),
        scratch_shapes=[
            pltpu.VMEM((2, ATTN_WIDTH, ts), BF16),
            pltpu.VMEM((2, BLOCK + ts, KV_WIDTH), BF16),
            pltpu.VMEM((2, KV_WIDTH, BLOCK + ts), BF16),
            pltpu.VMEM((3, ts, POOL_WIDTH), BF16),
            pltpu.VMEM((2, ts, ATTN_WIDTH), BF16),
            pltpu.VMEM((2, 2 * BLOCK, GQA_GROUP * BLOCK), F32),
            pltpu.VMEM((BLOCK, KV_WIDTH), BF16),
            pltpu.VMEM((KV_WIDTH, BLOCK), BF16),
            pltpu.VMEM((MAX_POOL_WINDOW + ts, POOL_WIDTH), F32),
        ],
        compiler_params=pltpu.CompilerParams(
            dimension_semantics=("arbitrary",), vmem_limit_bytes=VMEM_LIMIT_BYTES),
        name="token_mixer",
    )(sinks.astype(F32), x, x, mod, mod, positions.reshape(batch, 1, seq), norm_g.reshape(1, d), wqkv_t, wu,
      gq, gk, freq, pool_w.astype(BF16), pool_scale.reshape(1, POOL_WIDTH),
      w_o[:ATTN_WIDTH].astype(BF16), w_o[ATTN_WIDTH:].astype(BF16))


def _mlp_kernel(x_ref, mod_ref, gain_ref, w1_ref, w2_ref, o_ref):
    x = x_ref[0]
    mod = mod_ref[0]
    h = _modulated_rms_norm(x, gain_ref[...], mod[4:5], mod[3:4]).astype(BF16)
    acc = None
    fc = MLP_FF_CHUNK
    for c in range(w1_ref.shape[1] // fc):
        a = jnp.maximum(jnp.dot(h, w1_ref[:, fc * c:fc * (c + 1)], preferred_element_type=F32), 0.0)
        part = jnp.dot((a * a).astype(BF16), w2_ref[fc * c:fc * (c + 1), :], preferred_element_type=F32)
        acc = part if acc is None else acc + part
    o_ref[0] = x + mod[5:6] * acc


def _mlp(x, mod, norm_g, w1, w2):
    batch, seq, d = x.shape
    d_ff = w1.shape[1]
    tm = MLP_TOKENS
    assert seq % tm == 0 and d_ff % MLP_FF_CHUNK == 0
    return pl.pallas_call(
        _mlp_kernel,
        grid=(batch, seq // tm),
        in_specs=[
            pl.BlockSpec((1, tm, d), lambda b, t: (b, t, 0)),
            pl.BlockSpec((1, N_MOD, d), lambda b, t: (b, 0, 0)),
            pl.BlockSpec((1, d), lambda b, t: (0, 0)),
            pl.BlockSpec((d, d_ff), lambda b, t: (0, 0), pipeline_mode=pl.Buffered(1)),
            pl.BlockSpec((d_ff, d), lambda b, t: (0, 0), pipeline_mode=pl.Buffered(1)),
        ],
        out_specs=pl.BlockSpec((1, tm, d), lambda b, t: (b, t, 0)),
        out_shape=jax.ShapeDtypeStruct((batch, seq, d), F32),
        compiler_params=pltpu.CompilerParams(
            dimension_semantics=("arbitrary", "arbitrary"), vmem_limit_bytes=VMEM_LIMIT_BYTES),
        name="relu2_mlp",
    )(x, mod, norm_g.reshape(1, d), w1.astype(BF16), w2.astype(BF16))


def kernel(x, c, positions, w_ada, b_ada, norm1_g, w_in, q_norm_g, k_norm_g, attn_sinks,
           pool_w, pool_scale, w_o, norm2_g, w_mlp_in, w_mlp_out):
    h_res = x
    for l in range(w_ada.shape[0]):
        mod = _ada_modulation(c, w_ada[l], b_ada[l]).transpose(1, 0, 2)
        h_res = _mixer(h_res, mod, positions, norm1_g[l], w_in[l], q_norm_g[l], k_norm_g[l],
                       attn_sinks[l], pool_w[l], pool_scale[l], w_o[l])
        h_res = _mlp(h_res, mod, norm2_g[l], w_mlp_in[l], w_mlp_out[l])
    return h_res
```
